```python
import jax, jax.numpy as jnp
from jax import lax
import numpy as np

D_MODEL = 4096
BATCH = 8
SEQ = 2048
DEPTH = 4

D_MIX = D_MODEL
HEAD_DIM = 128
W_CONV = D_MIX // 2
W_LRU = D_MIX - W_CONV
H_CONV = W_CONV // HEAD_DIM
H_LRU = W_LRU // HEAD_DIM
D_IN = 3 * W_CONV + 2 * W_LRU
K_CONV = 3
K_LRU = 4
LRU_C = 8.0
D_FF = -(-(8 * D_MODEL) // (3 * 256)) * 256
EPS = 1e-6

kernel_name = "hybrid_shortconv_rglru_swiglu"


def rms_norm(x, g):
    xf = x.astype(jnp.float32)
    y = xf * lax.rsqrt(jnp.mean(xf * xf, axis=-1, keepdims=True) + EPS)
    return (y * g.astype(jnp.float32)).astype(x.dtype)


def head_rms_norm(y, g, n_heads):
    b, s, w = y.shape
    yf = y.astype(jnp.float32).reshape(b, s, n_heads, w // n_heads)
    yf = yf * lax.rsqrt(jnp.mean(yf * yf, axis=-1, keepdims=True) + EPS)
    return (yf.reshape(b, s, w) * g.astype(jnp.float32)).astype(y.dtype)


def causal_depthwise_conv(x, w):
    k_width = w.shape[0]
    s = x.shape[1]
    xp = jnp.pad(x, ((0, 0), (k_width - 1, 0), (0, 0)))
    return sum(xp[:, k:k + s, :] * w[k] for k in range(k_width))


def rg_lru(xr, w_gate_r, b_gate_r, w_gate_i, b_gate_i, lru_lambda):
    b, s, w = xr.shape
    xh = xr.reshape(b, s, H_LRU, HEAD_DIM)
    r = jax.nn.sigmoid(jnp.einsum('bshd,hde->bshe', xh, w_gate_r).reshape(b, s, w) + b_gate_r)
    i = jax.nn.sigmoid(jnp.einsum('bshd,hde->bshe', xh, w_gate_i).reshape(b, s, w) + b_gate_i)
    log_a = -LRU_C * r.astype(jnp.float32) * jax.nn.softplus(-lru_lambda.astype(jnp.float32))
    a = jnp.exp(log_a)
    mult = jnp.sqrt(-jnp.expm1(2.0 * log_a))
    is_first = (jnp.arange(s) == 0)[None, :, None]
    mult = jnp.where(is_first, 1.0, mult)
    u = mult * (i * xr).astype(jnp.float32)

    def combine(c1, c2):
        a1, b1 = c1
        a2, b2 = c2
        return a1 * a2, a2 * b1 + b2

    _, h = lax.associative_scan(combine, (a, u), axis=1)
    return h.astype(xr.dtype)


def hybrid_mixer(u, w_in, conv_a, conv_b, conv_b_bias, w_gate_r, b_gate_r, w_gate_i, b_gate_i,
                 lru_lambda, out_norm_a, out_norm_b, w_out):
    proj = jnp.einsum('bsd,de->bse', u, w_in)
    g_b, g_c, x_a, x_r, g_y = jnp.split(
        proj, [W_CONV, 2 * W_CONV, 3 * W_CONV, 3 * W_CONV + W_LRU], axis=-1)
    y_a = g_b * causal_depthwise_conv(g_c * x_a, conv_a)
    x_r = causal_depthwise_conv(x_r, conv_b) + conv_b_bias
    y_b = rg_lru(x_r, w_gate_r, b_gate_r, w_gate_i, b_gate_i, lru_lambda) * jax.nn.gelu(g_y)
    y = jnp.concatenate([head_rms_norm(y_a, out_norm_a, H_CONV),
                         head_rms_norm(y_b, out_norm_b, H_LRU)], axis=-1)
    return jnp.einsum('bse,ed->bsd', y, w_out)


def swiglu_ffn(u, w_gate, w_up, w_down):
    h = jax.nn.silu(jnp.einsum('bsd,df->bsf', u, w_gate)) * jnp.einsum('bsd,df->bsf', u, w_up)
    return jnp.einsum('bsf,fd->bsd', h, w_down)


def setup_inputs(seed: int = 0) -> dict:
    key = jax.random.key(seed)
    ks = jax.random.split(key, 20)
    f32 = jnp.float32

    def nrm(k, shape, fan_in):
        return jax.random.normal(k, shape, f32) * (fan_in ** -0.5)

    def gain(k, shape):
        return 1.0 + 0.02 * jax.random.normal(k, shape, f32)

    def bias(k, shape):
        return 0.02 * jax.random.normal(k, shape, f32)

    a0 = jax.random.uniform(ks[10], (DEPTH, W_LRU), f32, 0.9, 0.999)
    s = a0 ** (1.0 / LRU_C)
    lru_lambda = jnp.log(s) - jnp.log1p(-s)

    return {
        "x": jax.random.normal(ks[0], (BATCH, SEQ, D_MODEL), f32),
        "norm_mix": gain(ks[1], (DEPTH, D_MODEL)),
        "w_in": nrm(ks[2], (DEPTH, D_MODEL, D_IN), D_MODEL),
        "conv_a": nrm(ks[3], (DEPTH, K_CONV, W_CONV), K_CONV),
        "conv_b": nrm(ks[4], (DEPTH, K_LRU, W_LRU), K_LRU),
        "conv_b_bias": bias(ks[5], (DEPTH, W_LRU)),
        "w_gate_r": nrm(ks[6], (DEPTH, H_LRU, HEAD_DIM, HEAD_DIM), HEAD_DIM),
        "b_gate_r": bias(ks[7], (DEPTH, W_LRU)),
        "w_gate_i": nrm(ks[8], (DEPTH, H_LRU, HEAD_DIM, HEAD_DIM), HEAD_DIM),
        "b_gate_i": bias(ks[9], (DEPTH, W_LRU)),
        "lru_lambda": lru_lambda,
        "out_norm_a": gain(ks[11], (DEPTH, W_CONV)),
        "out_norm_b": gain(ks[12], (DEPTH, W_LRU)),
        "w_out": nrm(ks[13], (DEPTH, D_MIX, D_MODEL), D_MIX),
        "norm_ffn": gain(ks[14], (DEPTH, D_MODEL)),
        "w_ffn_gate": nrm(ks[15], (DEPTH, D_MODEL, D_FF), D_MODEL),
        "w_ffn_up": nrm(ks[16], (DEPTH, D_MODEL, D_FF), D_MODEL),
        "w_ffn_down": nrm(ks[17], (DEPTH, D_FF, D_MODEL), D_FF),
        "final_norm": gain(ks[18], (D_MODEL,)),
    }


def reference(x, norm_mix, w_in, conv_a, conv_b, conv_b_bias, w_gate_r, b_gate_r, w_gate_i,
              b_gate_i, lru_lambda, out_norm_a, out_norm_b, w_out, norm_ffn, w_ffn_gate,
              w_ffn_up, w_ffn_down, final_norm):
    h = x
    for l in range(DEPTH):
        u = rms_norm(h, norm_mix[l])
        h = h + hybrid_mixer(u, w_in[l], conv_a[l], conv_b[l], conv_b_bias[l], w_gate_r[l],
                             b_gate_r[l], w_gate_i[l], b_gate_i[l], lru_lambda[l],
                             out_norm_a[l], out_norm_b[l], w_out[l])
        u = rms_norm(h, norm_ffn[l])
        h = h + swiglu_ffn(u, w_ffn_gate[l], w_ffn_up[l], w_ffn_down[l])
    return rms_norm(h, final_norm)
```

```python
import functools

import jax
import jax.numpy as jnp
from jax import lax
from jax.experimental import pallas as pl
from jax.experimental.pallas import tpu as pltpu

D_MODEL = 4096
HEAD_DIM = 128
W_CONV = D_MODEL // 2
W_LRU = D_MODEL - W_CONV
K_CONV = 3
K_LRU = 4
LRU_C = 8.0
EPS = 1e-6

LANES = 128
SUBLANES = 8
VMEM_LIMIT_BYTES = 56 * 1024 * 1024

F32 = jnp.float32
BF16 = jnp.bfloat16


def _params(semantics):
    return pltpu.CompilerParams(dimension_semantics=semantics,
                                vmem_limit_bytes=VMEM_LIMIT_BYTES)


def _dot(a, b):
    return jnp.dot(a, b, preferred_element_type=F32)


def _prep_kernel(x_ref, g_ref, v_ref, r_ref):
    x = x_ref[...]
    ms = jnp.mean(x * x, axis=-1, keepdims=True)
    v_ref[...] = (x * g_ref[...]).astype(BF16)
    r_ref[...] = jnp.broadcast_to(lax.rsqrt(ms + EPS), r_ref.shape)


def _prep(x, g, *, bm=256):
    t, d = x.shape
    return pl.pallas_call(
        _prep_kernel,
        grid=(t // bm,),
        in_specs=[pl.BlockSpec((bm, d), lambda m: (m, 0)),
                  pl.BlockSpec((1, d), lambda m: (0, 0))],
        out_specs=[pl.BlockSpec((bm, d), lambda m: (m, 0)),
                   pl.BlockSpec((bm, LANES), lambda m: (m, 0))],
        out_shape=[jax.ShapeDtypeStruct((t, d), BF16),
                   jax.ShapeDtypeStruct((t, LANES), F32)],
        compiler_params=_params(("arbitrary",)),
        name="prep_norm",
    )(x, g.reshape(1, d))


def _final_kernel(h_ref, r_ref, g_ref, o_ref):
    for c in range(h_ref.shape[1] // LANES):
        sl = slice(c * LANES, (c + 1) * LANES)
        o_ref[:, sl] = h_ref[:, sl] * r_ref[...] * g_ref[:, sl]


def _final_norm(h, rstd, g, *, bm=256):
    t, d = h.shape
    return pl.pallas_call(
        _final_kernel,
        grid=(t // bm,),
        in_specs=[pl.BlockSpec((bm, d), lambda m: (m, 0)),
                  pl.BlockSpec((bm, LANES), lambda m: (m, 0)),
                  pl.BlockSpec((1, d), lambda m: (0, 0))],
        out_specs=pl.BlockSpec((bm, d), lambda m: (m, 0)),
        out_shape=jax.ShapeDtypeStruct((t, d), F32),
        compiler_params=_params(("arbitrary",)),
        name="final_norm",
    )(h, rstd, g.reshape(1, d))


def _head_norm(y, gain):
    ms = jnp.mean(y * y, axis=-1, keepdims=True)
    return y * lax.rsqrt(ms + EPS) * gain


def _shifted(buf, sl, shift, bm):
    return buf[pl.ds(SUBLANES - shift, bm), sl]


def _mixer_a_kernel(v_ref, r_ref, wb_ref, wc_ref, wx_ref, cw_ref, gain_ref,
                    y_ref, zbuf, *, tiles_per_seq):
    m = pl.program_id(1)
    bm, bn = y_ref.shape

    @pl.when(m % tiles_per_seq == 0)
    def _():
        zbuf[0:SUBLANES, :] = jnp.zeros((SUBLANES, bn), F32)

    v = v_ref[...]
    rstd = r_ref[...]
    gb = _dot(v, wb_ref[...])
    gc = _dot(v, wc_ref[...])
    xa = _dot(v, wx_ref[...])
    for c in range(bn // LANES):
        sl = slice(c * LANES, (c + 1) * LANES)
        z = (gc[:, sl] * rstd) * (xa[:, sl] * rstd)
        zbuf[pl.ds(SUBLANES, bm), sl] = z
        conv = (cw_ref[2:3, sl] * z
                + cw_ref[1:2, sl] * _shifted(zbuf, sl, 1, bm)
                + cw_ref[0:1, sl] * _shifted(zbuf, sl, 2, bm))
        ya = (gb[:, sl] * rstd) * conv
        y_ref[:, sl] = _head_norm(ya, gain_ref[:, sl]).astype(BF16)
    zbuf[0:SUBLANES, :] = zbuf[pl.ds(bm, SUBLANES), :]


def _mixer_a(v, rstd, w_in, conv_a, gain, *, seq, bm=512, bn=256):
    t, d = v.shape
    nb = W_CONV // bn
    kernel = functools.partial(_mixer_a_kernel, tiles_per_seq=seq // bm)
    wspec = lambda off: pl.BlockSpec((d, bn), lambda j, m: (0, off + j))
    return pl.pallas_call(
        kernel,
        grid=(nb, t // bm),
        in_specs=[pl.BlockSpec((bm, d), lambda j, m: (m, 0)),
                  pl.BlockSpec((bm, LANES), lambda j, m: (m, 0)),
                  wspec(0), wspec(nb), wspec(2 * nb),
                  pl.BlockSpec((K_CONV, bn), lambda j, m: (0, j)),
                  pl.BlockSpec((1, bn), lambda j, m: (0, j))],
        out_specs=pl.BlockSpec((bm, bn), lambda j, m: (m, j)),
        out_shape=jax.ShapeDtypeStruct((t, W_CONV), BF16),
        scratch_shapes=[pltpu.VMEM((bm + SUBLANES, bn), F32)],
        compiler_params=_params(("arbitrary", "arbitrary")),
        name="mixer_conv",
    )(v, rstd, w_in, w_in, w_in, conv_a, gain.reshape(1, W_CONV))


def _softplus(x):
    return jnp.maximum(x, 0.0) + jnp.log1p(jnp.exp(-jnp.abs(x)))


def _gelu_tanh(x):
    c = 0.7978845608028654
    return 0.5 * x * (1.0 + jnp.tanh(c * (x + 0.044715 * (x * x * x))))


def _sigmoid(x):
    return 0.5 * (jnp.tanh(0.5 * x) + 1.0)


def _lru_scan(a, b, h_in):
    bm = a.shape[0]
    groups = bm // SUBLANES
    a3 = a.reshape(groups, SUBLANES, LANES)
    b3 = b.reshape(groups, SUBLANES, LANES)
    row = lax.broadcasted_iota(jnp.int32, a3.shape, 1)
    for k in (1, 2, 4):
        keep = row >= k
        a_prev = pltpu.roll(a3, k, 1)
        b_prev = pltpu.roll(b3, k, 1)
        b3 = jnp.where(keep, a3 * b_prev + b3, b3)
        a3 = jnp.where(keep, a3 * a_prev, a3)
    a_tot = jnp.broadcast_to(a3[:, SUBLANES - 1:SUBLANES, :], a3.shape)
    b_tot = jnp.broadcast_to(b3[:, SUBLANES - 1:SUBLANES, :], b3.shape)
    h = h_in
    entering = []
    for g in range(groups):
        entering.append(h)
        h = a_tot[g] * h + b_tot[g]
    h_prev = jnp.stack(entering)
    out = a3 * h_prev + b3
    return out.reshape(bm, LANES), h


def _mixer_b_kernel(v_ref, r_ref, wx_ref, wg_ref, cw_ref, cb_ref, wri_ref,
                    br_ref, bi_ref, lam_ref, gain_ref, y_ref, xbuf, hbuf, *,
                    tiles_per_seq):
    m = pl.program_id(1)
    bm, bn = y_ref.shape
    seq_start = m % tiles_per_seq == 0

    @pl.when(seq_start)
    def _():
        xbuf[0:SUBLANES, :] = jnp.zeros((SUBLANES, bn), F32)
        hbuf[...] = jnp.zeros(hbuf.shape, F32)

    v = v_ref[...]
    rstd = r_ref[...]
    xr_all = _dot(v, wx_ref[...])
    gy_all = _dot(v, wg_ref[...])
    first_row = jnp.logical_and(
        lax.broadcasted_iota(jnp.int32, (bm, LANES), 0) == 0, seq_start)
    for c in range(bn // LANES):
        sl = slice(c * LANES, (c + 1) * LANES)
        x = xr_all[:, sl] * rstd
        xbuf[pl.ds(SUBLANES, bm), sl] = x
        xc = (cw_ref[3:4, sl] * x
              + cw_ref[2:3, sl] * _shifted(xbuf, sl, 1, bm)
              + cw_ref[1:2, sl] * _shifted(xbuf, sl, 2, bm)
              + cw_ref[0:1, sl] * _shifted(xbuf, sl, 3, bm)
              + cb_ref[:, sl])
        gates = _dot(xc.astype(BF16), wri_ref[c])
        r_gate = _sigmoid(gates[:, :LANES] + br_ref[:, sl])
        i_gate = _sigmoid(gates[:, LANES:] + bi_ref[:, sl])
        log_a = (-LRU_C * r_gate) * _softplus(-lam_ref[:, sl])
        a = jnp.exp(log_a)
        mult = jnp.sqrt(-jnp.tanh(log_a) * (1.0 + a * a))
        mult = jnp.where(first_row, 1.0, mult)
        b = mult * (i_gate * xc)
        h, h_out = _lru_scan(a, b, hbuf[:, sl])
        hbuf[:, sl] = h_out
        yb = h * _gelu_tanh(gy_all[:, sl] * rstd)
        y_ref[:, sl] = _head_norm(yb, gain_ref[:, sl]).astype(BF16)
    xbuf[0:SUBLANES, :] = xbuf[pl.ds(bm, SUBLANES), :]


def _mixer_b(v, rstd, w_in, conv_b, conv_b_bias, w_ri, b_r, b_i, lam, gain, *,
             seq, bm=512, bn=256):
    t, d = v.shape
    nb = W_LRU // bn
    hb = bn // HEAD_DIM
    col0 = 3 * W_CONV // bn
    kernel = functools.partial(_mixer_b_kernel, tiles_per_seq=seq // bm)
    wspec = lambda off: pl.BlockSpec((d, bn), lambda j, m: (0, off + j))
    vec = pl.BlockSpec((1, bn), lambda j, m: (0, j))
    row = lambda a: a.reshape(1, W_LRU)
    return pl.pallas_call(
        kernel,
        grid=(nb, t // bm),
        in_specs=[pl.BlockSpec((bm, d), lambda j, m: (m, 0)),
                  pl.BlockSpec((bm, LANES), lambda j, m: (m, 0)),
                  wspec(col0), wspec(col0 + nb),
                  pl.BlockSpec((K_LRU, bn), lambda j, m: (0, j)),
                  vec,
                  pl.BlockSpec((hb, HEAD_DIM, 2 * HEAD_DIM),
                               lambda j, m: (j, 0, 0)),
                  vec, vec, vec, vec],
        out_specs=pl.BlockSpec((bm, bn), lambda j, m: (m, j)),
        out_shape=jax.ShapeDtypeStruct((t, W_LRU), BF16),
        scratch_shapes=[pltpu.VMEM((bm + SUBLANES, bn), F32),
                        pltpu.VMEM((SUBLANES, bn), F32)],
        compiler_params=_params(("arbitrary", "arbitrary")),
        name="mixer_lru",
    )(v, rstd, w_in, w_in, conv_b, row(conv_b_bias), w_ri, row(b_r), row(b_i),
      row(lam), row(gain))


def _proj_res_kernel(*refs, n_in, d_total):
    x_refs = refs[:n_in]
    w_refs = refs[n_in:2 * n_in]
    h_ref, g_ref, ho_ref, v_ref, r_ref, ssq = refs[2 * n_in:]
    n = pl.program_id(1)

    @pl.when(n == 0)
    def _():
        ssq[...] = jnp.zeros(ssq.shape, F32)

    acc = _dot(x_refs[0][...], w_refs[0][...])
    for x_ref, w_ref in zip(x_refs[1:], w_refs[1:]):
        acc = acc + _dot(x_ref[...], w_ref[...])
    hn = h_ref[...] + acc
    ho_ref[...] = hn
    v_ref[...] = (hn * g_ref[...]).astype(BF16)
    ssq[...] += jnp.sum(hn * hn, axis=-1, keepdims=True)

    @pl.when(n == pl.num_programs(1) - 1)
    def _():
        r_ref[...] = jnp.broadcast_to(
            lax.rsqrt(ssq[...] * (1.0 / d_total) + EPS), r_ref.shape)


def _proj_res(xs, w, h, g, *, bm, bn):
    t, d = h.shape
    n_in = len(xs)
    k_i = xs[0].shape[1]
    assert all(x.shape[1] == k_i for x in xs) and n_in * k_i == w.shape[0]
    kernel = functools.partial(_proj_res_kernel, n_in=n_in, d_total=d)
    x_specs = [pl.BlockSpec((bm, k_i), lambda m, n: (m, 0)) for _ in xs]
    w_specs = [pl.BlockSpec((k_i, bn), lambda m, n, i=i: (i, n))
               for i in range(n_in)]
    return pl.pallas_call(
        kernel,
        grid=(t // bm, d // bn),
        in_specs=x_specs + w_specs + [
            pl.BlockSpec((bm, bn), lambda m, n: (m, n)),
            pl.BlockSpec((1, bn), lambda m, n: (0, n))],
        out_specs=[pl.BlockSpec((bm, bn), lambda m, n: (m, n)),
                   pl.BlockSpec((bm, bn), lambda m, n: (m, n)),
                   pl.BlockSpec((bm, LANES), lambda m, n: (m, 0))],
        out_shape=[jax.ShapeDtypeStruct((t, d), F32),
                   jax.ShapeDtypeStruct((t, d), BF16),
                   jax.ShapeDtypeStruct((t, LANES), F32)],
        scratch_shapes=[pltpu.VMEM((bm, 1), F32)],
        compiler_params=_params(("arbitrary", "arbitrary")),
        name=f"proj_res_{n_in}",
    )(*xs, *([w] * n_in), h, g.reshape(1, d))


def _ffn_up_kernel(v_ref, r_ref, wg_ref, wu_ref, o_ref):
    v = v_ref[...]
    rstd = r_ref[...]
    gate = _dot(v, wg_ref[...])
    up = _dot(v, wu_ref[...])
    for c in range(o_ref.shape[1] // LANES):
        sl = slice(c * LANES, (c + 1) * LANES)
        gt = gate[:, sl] * rstd
        o_ref[:, sl] = ((gt * _sigmoid(gt)) * (up[:, sl] * rstd)).astype(BF16)


def _ffn_up(v, rstd, w_gate, w_up, *, bm=1024, bn=256):
    t, d = v.shape
    f = w_gate.shape[1]
    return pl.pallas_call(
        _ffn_up_kernel,
        grid=(t // bm, f // bn),
        in_specs=[pl.BlockSpec((bm, d), lambda m, n: (m, 0)),
                  pl.BlockSpec((bm, LANES), lambda m, n: (m, 0)),
                  pl.BlockSpec((d, bn), lambda m, n: (0, n)),
                  pl.BlockSpec((d, bn), lambda m, n: (0, n))],
        out_specs=pl.BlockSpec((bm, bn), lambda m, n: (m, n)),
        out_shape=jax.ShapeDtypeStruct((t, f), BF16),
        compiler_params=_params(("arbitrary", "arbitrary")),
        name="ffn_up",
    )(v, rstd, w_gate, w_up)


def kernel(x, norm_mix, w_in, conv_a, conv_b, conv_b_bias, w_gate_r, b_gate_r, w_gate_i, b_gate_i, lru_lambda, out_norm_a, out_norm_b, w_out, norm_ffn, w_ffn_gate, w_ffn_up, w_ffn_down, final_norm):
    batch, seq, d = x.shape
    depth = w_in.shape[0]
    h = x.reshape(batch * seq, d)

    v, rstd = _prep(h, norm_mix[0])
    for l in range(depth):
        w_in_b = w_in[l].astype(BF16)
        w_ri_b = jnp.concatenate([w_gate_r[l], w_gate_i[l]],
                                 axis=-1).astype(BF16)
        y_a = _mixer_a(v, rstd, w_in_b, conv_a[l], out_norm_a[l], seq=seq)
        y_b = _mixer_b(v, rstd, w_in_b, conv_b[l], conv_b_bias[l],
                       w_ri_b, b_gate_r[l], b_gate_i[l], lru_lambda[l],
                       out_norm_b[l], seq=seq)
        h, v, rstd = _proj_res([y_a, y_b], w_out[l].astype(BF16), h,
                               norm_ffn[l], bm=1024, bn=512)
        hmid = _ffn_up(v, rstd, w_ffn_gate[l].astype(BF16),
                       w_ffn_up[l].astype(BF16))
        g_next = norm_mix[l + 1] if l + 1 < depth else final_norm
        h, v, rstd = _proj_res([hmid], w_ffn_down[l].astype(BF16), h, g_next,
                               bm=512, bn=512)
    out = _final_norm(h, rstd, final_norm)
    return out.reshape(batch, seq, d)
```
